```python
import math
import jax, jax.numpy as jnp
from jax import lax
import numpy as np

D_MODEL = 1024
BATCH = 8
SEQ = 4096
DEPTH = 4
DEC_BATCH = 1
DEC_SEQ = 16384
PAST_LEN = 128

HEAD_DIM = 64
A_Q_HEADS = 6
A_KV_HEADS = 2
A_HALF_WINDOW = 128
B_GROUPS = ((128, 1), (512, 4), (2048, 16))
B_HEADS_PER_GROUP = 4
C_HEADS = 6
GRID_W = 64
NA_ROWS = 8
NA_COLS = 16
ROPE_THETA = 500000.0
ROPE_DIM = HEAD_DIM // 4
D_FF = 4 * D_MODEL
NORM_EPS = 1e-6
N_BRANCHES = 3
NEG = -1e30

A_Q = A_Q_HEADS * HEAD_DIM
A_KV = A_KV_HEADS * HEAD_DIM
B_W = len(B_GROUPS) * B_HEADS_PER_GROUP * HEAD_DIM
B_OUT = B_HEADS_PER_GROUP * HEAD_DIM
C_W = C_HEADS * HEAD_DIM
SPLITS = (A_Q, A_KV, A_KV, B_W, B_W, B_W, C_W, C_W, C_W, N_BRANCHES * D_MODEL)
D_IN = sum(SPLITS)

kernel_name = 'hybrid_gated_sparse_attention_encoder'


def rmsnorm(x, g):
    xf = x.astype(jnp.float32)
    y = xf * lax.rsqrt(jnp.mean(xf * xf, axis=-1, keepdims=True) + NORM_EPS)
    return (y * g.astype(jnp.float32)).astype(x.dtype)


def rope_tables(s):
    inv = ROPE_THETA ** (-jnp.arange(0, ROPE_DIM, 2, dtype=jnp.float32) / ROPE_DIM)
    ang = jnp.arange(s, dtype=jnp.float32)[:, None] * inv[None, :]
    return jnp.cos(ang), jnp.sin(ang)


def apply_rope(t, cos, sin):
    half = ROPE_DIM // 2
    tf = t.astype(jnp.float32)
    c = cos[None, :, None, :]
    sn = sin[None, :, None, :]
    t1 = tf[..., :half]
    t2 = tf[..., half:ROPE_DIM]
    out = jnp.concatenate([t1 * c - t2 * sn, t2 * c + t1 * sn, tf[..., ROPE_DIM:]], axis=-1)
    return out.astype(t.dtype)


def band_attention(q, k, v, half_w):
    n, l, hk, g, d = q.shape
    blk = half_w
    nb = -(-l // blk)
    pad = nb * blk - l
    qp = jnp.pad(q, ((0, 0), (0, pad), (0, 0), (0, 0), (0, 0))).reshape(n, nb, blk, hk, g, d)

    def kv_blocks(t):
        tp = jnp.pad(t, ((0, 0), (blk, blk + pad), (0, 0), (0, 0))).reshape(n, nb + 2, blk, hk, d)
        return jnp.concatenate([tp[:, :-2], tp[:, 1:-1], tp[:, 2:]], axis=2)

    kb = kv_blocks(k)
    vb = kv_blocks(v)
    qpos = jnp.arange(nb)[:, None] * blk + jnp.arange(blk)[None, :]
    kpos = (jnp.arange(nb)[:, None] - 1) * blk + jnp.arange(3 * blk)[None, :]
    rel = kpos[:, None, :] - qpos[:, :, None]
    mask = (jnp.abs(rel) <= half_w) & (kpos[:, None, :] >= 0) & (kpos[:, None, :] < l)
    s = jnp.einsum('nbqhgd,nbkhd->nbhgqk', qp, kb, preferred_element_type=jnp.float32)
    s = jnp.where(mask[None, :, None, None], s, NEG)
    m = jnp.max(s, axis=-1, keepdims=True)
    p = jnp.exp(s - m)
    den = jnp.sum(p, axis=-1, keepdims=True)
    o = jnp.einsum('nbhgqk,nbkhd->nbqhgd', (p / den).astype(v.dtype), vb,
                   preferred_element_type=jnp.float32)
    lse = (m + jnp.log(den))[..., 0]
    o = o.reshape(n, nb * blk, hk, g, d)[:, :l].astype(q.dtype)
    lse = jnp.moveaxis(lse, -1, 2).reshape(n, nb * blk, hk, g)[:, :l]
    return o, lse


def windowed_gqa_sink(qa, ka, va, sink, cos, sin):
    b, s, _ = qa.shape
    grp = A_Q_HEADS // A_KV_HEADS
    q = apply_rope(qa.reshape(b, s, A_Q_HEADS, HEAD_DIM), cos, sin) * (HEAD_DIM ** -0.5)
    k = apply_rope(ka.reshape(b, s, A_KV_HEADS, HEAD_DIM), cos, sin)
    v = va.reshape(b, s, A_KV_HEADS, HEAD_DIM)
    o, lse = band_attention(q.reshape(b, s, A_KV_HEADS, grp, HEAD_DIM), k, v, A_HALF_WINDOW)
    factor = jax.nn.sigmoid(lse - sink.reshape(A_KV_HEADS, grp).astype(jnp.float32))
    return (o.astype(jnp.float32) * factor[..., None]).reshape(b, s, A_Q).astype(qa.dtype)


def dilated_mixture(qb, kb, vb, cos, sin):
    b, s, _ = qb.shape
    ng = len(B_GROUPS)
    q = (apply_rope(qb.reshape(b, s, ng * B_HEADS_PER_GROUP, HEAD_DIM), cos, sin)
         * (HEAD_DIM ** -0.5)).reshape(b, s, ng, B_HEADS_PER_GROUP, HEAD_DIM)
    k = apply_rope(kb.reshape(b, s, ng * B_HEADS_PER_GROUP, HEAD_DIM), cos, sin).reshape(
        b, s, ng, B_HEADS_PER_GROUP, HEAD_DIM)
    v = vb.reshape(b, s, ng, B_HEADS_PER_GROUP, HEAD_DIM)
    outs = []
    lses = []
    for gi, (window, dil) in enumerate(B_GROUPS):
        half = window // (2 * dil)
        sub = s // dil

        def strided(t):
            return t.reshape(b, sub, dil, B_HEADS_PER_GROUP, HEAD_DIM).transpose(0, 2, 1, 3, 4).reshape(
                b * dil, sub, B_HEADS_PER_GROUP, HEAD_DIM)

        o, lse = band_attention(strided(q[:, :, gi])[:, :, :, None, :], strided(k[:, :, gi]),
                                strided(v[:, :, gi]), half)
        o = o[:, :, :, 0].reshape(b, dil, sub, B_HEADS_PER_GROUP, HEAD_DIM).transpose(0, 2, 1, 3, 4).reshape(
            b, s, B_HEADS_PER_GROUP, HEAD_DIM)
        lse = lse[..., 0].reshape(b, dil, sub, B_HEADS_PER_GROUP).transpose(0, 2, 1, 3).reshape(
            b, s, B_HEADS_PER_GROUP)
        outs.append(o)
        lses.append(lse)
    wts = jax.nn.softmax(jnp.stack(lses), axis=0)
    o = jnp.einsum('gbsh,gbshd->bshd', wts, jnp.stack(outs).astype(jnp.float32))
    return o.reshape(b, s, B_OUT).astype(qb.dtype)


def neighborhood_attention(qc, kc, vc, rpb):
    b, s, _ = qc.shape
    rows = s // GRID_W
    kr = min(NA_ROWS, rows)
    qg = (qc * (HEAD_DIM ** -0.5)).reshape(b, rows, GRID_W, C_HEADS, HEAD_DIM)
    kg = kc.reshape(b, rows, GRID_W, C_HEADS, HEAD_DIM)
    vg = vc.reshape(b, rows, GRID_W, C_HEADS, HEAD_DIM)
    r = jnp.arange(rows)
    rstart = jnp.clip(r - kr // 2, 0, rows - kr)
    ridx = rstart[:, None] + jnp.arange(kr)[None, :]
    kn = jnp.take(kg, ridx, axis=1)
    vn = jnp.take(vg, ridx, axis=1)
    c = jnp.arange(GRID_W)
    cstart = jnp.clip(c - NA_COLS // 2, 0, GRID_W - NA_COLS)
    colmask = (c[None, :] >= cstart[:, None]) & (c[None, :] < cstart[:, None] + NA_COLS)
    roff = ridx - r[:, None] + (NA_ROWS - 1)
    coff = jnp.clip(c[None, :] - c[:, None] + (NA_COLS - 1), 0, 2 * NA_COLS - 2)
    bias = rpb[:, roff][:, :, :, coff].transpose(1, 0, 3, 2, 4)
    sc = jnp.einsum('brqhd,brjchd->brhqjc', qg, kn, preferred_element_type=jnp.float32)
    sc = jnp.where(colmask[:, None, :], sc + bias[None].astype(jnp.float32), NEG)
    p = jax.nn.softmax(sc.reshape(b, rows, C_HEADS, GRID_W, kr * GRID_W), axis=-1).reshape(sc.shape)
    o = jnp.einsum('brhqjc,brjchd->brqhd', p.astype(vn.dtype), vn, preferred_element_type=jnp.float32)
    return o.reshape(b, s, C_W).astype(qc.dtype)


def trunk(x, norm1_g, w_in, sink_a, rpb_c, w_pa, w_pb, w_pc, w_o, norm2_g, w_up, w_down, final_g):
    b, s, _ = x.shape
    cos, sin = rope_tables(s)
    cuts = [sum(SPLITS[:i + 1]) for i in range(len(SPLITS) - 1)]
    for layer in range(DEPTH):
        hn = rmsnorm(x, norm1_g[layer])
        proj = hn @ w_in[layer]
        qa, ka, va, qb, kb, vb, qc, kc, vc, gl = jnp.split(proj, cuts, axis=-1)
        ya = windowed_gqa_sink(qa, ka, va, sink_a[layer], cos, sin) @ w_pa[layer]
        yb = dilated_mixture(qb, kb, vb, cos, sin) @ w_pb[layer]
        yc = neighborhood_attention(qc, kc, vc, rpb_c[layer]) @ w_pc[layer]
        gates = jax.nn.sigmoid(gl.astype(jnp.float32)).reshape(b, s, N_BRANCHES, D_MODEL)
        mix = (gates[:, :, 0] * ya.astype(jnp.float32) + gates[:, :, 1] * yb.astype(jnp.float32)
               + gates[:, :, 2] * yc.astype(jnp.float32))
        x = x + mix.astype(x.dtype) @ w_o[layer]
        hn = rmsnorm(x, norm2_g[layer])
        x = x + jnp.square(jax.nn.relu(hn @ w_up[layer])) @ w_down[layer]
    return rmsnorm(x, final_g)


def setup_inputs(seed: int = 0) -> dict:
    key = jax.random.key(seed)
    ks = jax.random.split(key, 16)
    f32 = jnp.float32

    def nrm(k, shape, scale):
        return jax.random.normal(k, shape, f32) * scale

    return {
        'x_prompt': nrm(ks[0], (BATCH, SEQ, D_MODEL), 1.0),
        'x_sample': nrm(ks[1], (DEC_BATCH, DEC_SEQ, D_MODEL), 1.0),
        'norm1_g': 1.0 + nrm(ks[2], (DEPTH, D_MODEL), 0.05),
        'w_in': nrm(ks[3], (DEPTH, D_MODEL, D_IN), D_MODEL ** -0.5),
        'sink_a': nrm(ks[4], (DEPTH, A_Q_HEADS), 0.5),
        'rpb_c': nrm(ks[5], (DEPTH, C_HEADS, 2 * NA_ROWS - 1, 2 * NA_COLS - 1), 0.1),
        'w_pa': nrm(ks[6], (DEPTH, A_Q, D_MODEL), A_Q ** -0.5),
        'w_pb': nrm(ks[7], (DEPTH, B_OUT, D_MODEL), B_OUT ** -0.5),
        'w_pc': nrm(ks[8], (DEPTH, C_W, D_MODEL), C_W ** -0.5),
        'w_o': nrm(ks[9], (DEPTH, D_MODEL, D_MODEL), D_MODEL ** -0.5),
        'norm2_g': 1.0 + nrm(ks[10], (DEPTH, D_MODEL), 0.05),
        'w_up': nrm(ks[11], (DEPTH, D_MODEL, D_FF), D_MODEL ** -0.5),
        'w_down': nrm(ks[12], (DEPTH, D_FF, D_MODEL), D_FF ** -0.5),
        'final_g': 1.0 + nrm(ks[13], (D_MODEL,), 0.05),
    }


def reference(x_prompt, x_sample, norm1_g, w_in, sink_a, rpb_c, w_pa, w_pb, w_pc, w_o, norm2_g, w_up, w_down,
              final_g):
    y_prompt = trunk(x_prompt, norm1_g, w_in, sink_a, rpb_c, w_pa, w_pb, w_pc, w_o, norm2_g, w_up, w_down, final_g)
    y_sample = trunk(x_sample, norm1_g, w_in, sink_a, rpb_c, w_pa, w_pb, w_pc, w_o, norm2_g, w_up, w_down, final_g)
    return (y_prompt, y_sample)
```

```python
import functools

import jax
import jax.numpy as jnp
from jax import lax
from jax.experimental import pallas as pl
from jax.experimental.pallas import tpu as pltpu

D_MODEL = 1024
DEPTH = 4
HEAD_DIM = 64
A_Q_HEADS = 6
A_KV_HEADS = 2
A_HALF_WINDOW = 128
B_GROUPS = ((128, 1), (512, 4), (2048, 16))
B_HEADS_PER_GROUP = 4
C_HEADS = 6
GRID_W = 64
NA_ROWS = 8
NA_COLS = 16
ROPE_THETA = 500000.0
ROPE_DIM = HEAD_DIM // 4
D_FF = 4 * D_MODEL
NORM_EPS = 1e-6
N_BRANCHES = 3
NEG = -1e30

A_Q = A_Q_HEADS * HEAD_DIM
A_KV = A_KV_HEADS * HEAD_DIM
A_W = A_Q + 2 * A_KV
B_G = B_HEADS_PER_GROUP * HEAD_DIM
B_W = len(B_GROUPS) * 3 * B_G
C_Q = C_HEADS * HEAD_DIM
C_W = 3 * C_Q
QKV_W = A_W + B_W + C_W
GATE_W = N_BRANCHES * D_MODEL
Q_SCALE = HEAD_DIM ** -0.5

LANES = 128
HALF = LANES // 2
VMEM_LIMIT_BYTES = 56 * 1024 * 1024
TOKEN_TILE = 512
QKV_CHUNK = 512
C_ROWS_PER_STEP = 8

F32 = jnp.float32
BF16 = jnp.bfloat16


def _params(n_axes):
    return pltpu.CompilerParams(dimension_semantics=("arbitrary",) * n_axes,
                                vmem_limit_bytes=VMEM_LIMIT_BYTES)


def _const_spec(shape):
    zeros = (0,) * len(shape)
    return pl.BlockSpec(shape, lambda *_: zeros)


def _rms(x, g):
    return x * lax.rsqrt(jnp.mean(x * x, axis=-1, keepdims=True) + NORM_EPS) * g


def _sigmoid(z):
    return 1.0 / (1.0 + jnp.exp(-z))


def _nt_dot(a, b):
    return lax.dot_general(a, b, (((1,), (1,)), ((), ())), preferred_element_type=F32)


def _softmax_rows(sc):
    m = jnp.max(sc, axis=-1, keepdims=True)
    e = jnp.exp(sc - m)
    den = jnp.sum(e, axis=-1, keepdims=True)
    return (e * (1.0 / den)).astype(BF16), m + jnp.log(den)


def _qkv_plan():
    plan = []
    for n in range(A_W // LANES):
        kind = 0 if n < A_Q // LANES else (1 if n < (A_Q + A_KV) // LANES else 2)
        plan.append((0, n * LANES, kind < 2, kind == 0))
    for n in range(B_W // LANES):
        kind = (n * LANES // B_G) % 3
        plan.append((1, n * LANES, kind < 2, kind == 0))
    for n in range(C_W // LANES):
        plan.append((2, n * LANES, False, n < C_Q // LANES))
    return plan


def _qkv_kernel(x_ref, g_ref, w_ref, rc_ref, rsa_ref, rsb_ref, a_ref, b_ref, c_ref):
    hn = _rms(x_ref[...], g_ref[...]).astype(BF16)
    rc, rsa, rsb = rc_ref[...], rsa_ref[...], rsb_ref[...]
    outs = (a_ref, b_ref, c_ref)
    plan = _qkv_plan()
    per_chunk = QKV_CHUNK // LANES
    for chunk in range(QKV_W // QKV_CHUNK):
        acc = jnp.dot(hn, w_ref[:, chunk * QKV_CHUNK:(chunk + 1) * QKV_CHUNK],
                      preferred_element_type=F32)
        for sub in range(per_chunk):
            dest, off, rope, scale = plan[chunk * per_chunk + sub]
            t = acc[:, sub * LANES:(sub + 1) * LANES]
            if rope:
                t = (t * rc + pltpu.roll(t, LANES - ROPE_DIM // 2, 1) * rsa
                     + pltpu.roll(t, ROPE_DIM // 2, 1) * rsb)
            if scale:
                t = t * Q_SCALE
            outs[dest][:, off:off + LANES] = t.astype(BF16)


def _qkv_proj(x, g, w, rope, seq):
    tokens = x.shape[0]
    tm = TOKEN_TILE
    pos_blocks = seq // tm
    rope_spec = pl.BlockSpec((tm, LANES), lambda i: (i % pos_blocks, 0))
    return pl.pallas_call(
        _qkv_kernel,
        grid=(tokens // tm,),
        in_specs=[pl.BlockSpec((tm, D_MODEL), lambda i: (i, 0)),
                  _const_spec((1, D_MODEL)),
                  _const_spec((D_MODEL, QKV_W)),
                  rope_spec, rope_spec, rope_spec],
        out_specs=[pl.BlockSpec((tm, A_W), lambda i: (i, 0)),
                   pl.BlockSpec((tm, B_W), lambda i: (i, 0)),
                   pl.BlockSpec((tm, C_W), lambda i: (i, 0))],
        out_shape=[jax.ShapeDtypeStruct((tokens, A_W), BF16),
                   jax.ShapeDtypeStruct((tokens, B_W), BF16),
                   jax.ShapeDtypeStruct((tokens, C_W), BF16)],
        compiler_params=_params(1),
        name="qkv_proj",
    )(x, g, w, *rope)


def _lane_lo(shape):
    return lax.broadcasted_iota(jnp.int32, shape, len(shape) - 1) < HALF


def _head_pair(qc, keys, vals, valid, bias=None):
    lo_q = _lane_lo(qc.shape)
    outs, lses = [], []
    for half in range(2):
        qm = jnp.where(lo_q if half == 0 else ~lo_q, qc, jnp.zeros_like(qc))
        sc = _nt_dot(qm, keys[half])
        if bias is not None:
            sc = sc + bias[half]
        sc = jnp.where(valid, sc, NEG)
        pn, lse = _softmax_rows(sc)
        outs.append(jnp.dot(pn, vals[half], preferred_element_type=F32))
        lses.append(lse)
    lo_o = _lane_lo(outs[0].shape)
    return jnp.where(lo_o, outs[0], outs[1]), jnp.where(lo_o, lses[0], lses[1])


def _band_valid(rows, keys, halo, key0, length):
    row = lax.broadcasted_iota(jnp.int32, (rows, keys), 0)
    col = lax.broadcasted_iota(jnp.int32, (rows, keys), 1)
    kpos = key0 + col
    return (jnp.abs(col - halo - row) <= halo) & (kpos >= 0) & (kpos < length)


def _attn_a_kernel(q_ref, kl_ref, km_ref, kr_ref, vl_ref, vm_ref, vr_ref, sink_ref, o_ref,
                   *, tq, seq):
    i = pl.program_id(1)
    blk = A_HALF_WINDOW

    def swap_halves(t):
        return pltpu.roll(t.astype(F32), HALF, 1).astype(t.dtype)

    kcat = jnp.concatenate([kl_ref[...], km_ref[...], kr_ref[...]], axis=0)
    vcat = jnp.concatenate([vl_ref[...], vm_ref[...], vr_ref[...]], axis=0)
    ksw, vsw = swap_halves(kcat), swap_halves(vcat)
    grp = A_Q_HEADS // A_KV_HEADS
    for j in range(tq // blk):
        valid = _band_valid(blk, 3 * blk, blk, i * tq + (j - 1) * blk, seq)
        rows = slice(j * blk, (j + 1) * blk)
        win = slice(j * blk, (j + 3) * blk)
        for p in range(A_Q // LANES):
            keys, vals = [], []
            for half in range(2):
                aligned = ((2 * p + half) // grp) == half
                keys.append((kcat if aligned else ksw)[win])
                vals.append((vcat if aligned else vsw)[win])
            cols = slice(p * LANES, (p + 1) * LANES)
            o, lse = _head_pair(q_ref[rows, cols], keys, vals, valid)
            o_ref[rows, cols] = (o * _sigmoid(lse - sink_ref[:, cols])).astype(o_ref.dtype)


def _attn_a(qkv, sink_lanes, batch, seq):
    blk = A_HALF_WINDOW
    tq = 2 * blk
    per = tq // blk
    nblk = seq // blk
    kcol, vcol = A_Q // A_KV, A_Q // A_KV + 1

    def main(col):
        return pl.BlockSpec((None, tq, A_KV), lambda b, i: (b, i, col))

    def left(col):
        return pl.BlockSpec((None, blk, A_KV), lambda b, i: (b, jnp.maximum(i * per - 1, 0), col))

    def right(col):
        return pl.BlockSpec((None, blk, A_KV),
                            lambda b, i: (b, jnp.minimum((i + 1) * per, nblk - 1), col))

    x = qkv.reshape(batch, seq, A_W)
    out = pl.pallas_call(
        functools.partial(_attn_a_kernel, tq=tq, seq=seq),
        grid=(batch, seq // tq),
        in_specs=[pl.BlockSpec((None, tq, A_Q), lambda b, i: (b, i, 0)),
                  left(kcol), main(kcol), right(kcol),
                  left(vcol), main(vcol), right(vcol),
                  _const_spec((1, A_Q))],
        out_specs=pl.BlockSpec((None, tq, A_Q), lambda b, i: (b, i, 0)),
        out_shape=jax.ShapeDtypeStruct((batch, seq, A_Q), BF16),
        compiler_params=_params(2),
        name="attn_window",
    )(x, x, x, x, x, x, x, sink_lanes)
    return out.reshape(batch * seq, A_Q)


def _attn_b_kernel(q_ref, kl_ref, km_ref, kr_ref, vl_ref, vm_ref, vr_ref, o_ref, lse_ref,
                   *, tq, sub, halo):
    i = pl.program_id(2)
    blk = 2 * halo
    kcat = jnp.concatenate([kl_ref[...], km_ref[...], kr_ref[...]], axis=0)
    vcat = jnp.concatenate([vl_ref[...], vm_ref[...], vr_ref[...]], axis=0)
    for j in range(tq // blk):
        valid = _band_valid(blk, 2 * blk, halo, i * tq + j * blk - halo, sub)
        rows = slice(j * blk, (j + 1) * blk)
        win = slice(j * blk, (j + 2) * blk)
        for p in range(B_G // LANES):
            cols = slice(p * LANES, (p + 1) * LANES)
            kk, vv = kcat[win, cols], vcat[win, cols]
            o, lse = _head_pair(q_ref[rows, cols], (kk, kk), (vv, vv), valid)
            o_ref[rows, cols] = o
            lse_ref[rows, cols] = lse


def _attn_b_group(qkv, group, batch, seq):
    window, dil = B_GROUPS[group]
    halo = window // (2 * dil)
    sub = seq // dil
    tq = min(4 * halo, sub)
    per = tq // halo
    nhalo = sub // halo
    ncol = B_W // B_G
    qcol = 3 * group

    def main(col):
        return pl.BlockSpec((None, tq, B_G), lambda b, r, i: (b, i, r * ncol + col))

    def left(col):
        return pl.BlockSpec((None, halo, B_G),
                            lambda b, r, i: (b, jnp.maximum(i * per - 1, 0), r * ncol + col))

    def right(col):
        return pl.BlockSpec((None, halo, B_G),
                            lambda b, r, i: (b, jnp.minimum((i + 1) * per, nhalo - 1), r * ncol + col))

    x = qkv.reshape(batch, sub, dil * B_W)
    out_spec = pl.BlockSpec((None, tq, B_G), lambda b, r, i: (b, i, r))
    out_shape = jax.ShapeDtypeStruct((batch, sub, dil * B_G), F32)
    o, lse = pl.pallas_call(
        functools.partial(_attn_b_kernel, tq=tq, sub=sub, halo=halo),
        grid=(batch, dil, sub // tq),
        in_specs=[main(qcol),
                  left(qcol + 1), main(qcol + 1), right(qcol + 1),
                  left(qcol + 2), main(qcol + 2), right(qcol + 2)],
        out_specs=[out_spec, out_spec],
        out_shape=[out_shape, out_shape],
        compiler_params=_params(3),
        name=f"attn_dilated_{dil}",
    )(x, x, x, x, x, x, x)
    return o.reshape(batch * seq, B_G), lse.reshape(batch * seq, B_G)


def _attn_c_kernel(q_ref, kp_ref, kc_ref, kn_ref, vp_ref, vc_ref, vn_ref, bias_ref, o_ref,
                   kcat, vcat, *, rows):
    i = pl.program_id(1)
    step_tokens = C_ROWS_PER_STEP * GRID_W
    win = NA_ROWS * GRID_W
    for n, (k_ref, v_ref) in enumerate(((kp_ref, vp_ref), (kc_ref, vc_ref), (kn_ref, vn_ref))):
        kcat[n * step_tokens:(n + 1) * step_tokens] = k_ref[...]
        vcat[n * step_tokens:(n + 1) * step_tokens] = v_ref[...]
    qcol = lax.broadcasted_iota(jnp.int32, (GRID_W, win), 0)
    kcol = lax.broadcasted_iota(jnp.int32, (GRID_W, win), 1) % GRID_W
    cstart = jnp.clip(qcol - NA_COLS // 2, 0, GRID_W - NA_COLS)
    valid = (kcol >= cstart) & (kcol < cstart + NA_COLS)

    def row_body(rr, carry):
        r = i * C_ROWS_PER_STEP + rr
        rstart = jnp.clip(r - NA_ROWS // 2, 0, rows - NA_ROWS)
        start = rstart - r + (NA_ROWS - 1)
        off = pl.multiple_of(rstart * GRID_W - (i - 1) * step_tokens, GRID_W)
        qrows = pl.ds(pl.multiple_of(rr * GRID_W, GRID_W), GRID_W)
        for p in range(C_Q // LANES):
            cols = slice(p * LANES, (p + 1) * LANES)
            kk = kcat[pl.ds(off, win), cols]
            vv = vcat[pl.ds(off, win), cols]
            bias = (bias_ref[start, 2 * p], bias_ref[start, 2 * p + 1])
            o, _ = _head_pair(q_ref[qrows, cols], (kk, kk), (vv, vv), valid, bias)
            o_ref[qrows, cols] = o.astype(o_ref.dtype)
        return carry

    lax.fori_loop(0, C_ROWS_PER_STEP, row_body, 0)


def _attn_c(qkv, bias, batch, seq):
    rows = seq // GRID_W
    assert rows >= NA_ROWS and rows % C_ROWS_PER_STEP == 0
    step_tokens = C_ROWS_PER_STEP * GRID_W
    nsteps = rows // C_ROWS_PER_STEP

    def prev(col):
        return pl.BlockSpec((None, step_tokens, C_Q), lambda b, i: (b, jnp.maximum(i - 1, 0), col))

    def cur(col):
        return pl.BlockSpec((None, step_tokens, C_Q), lambda b, i: (b, i, col))

    def nxt(col):
        return pl.BlockSpec((None, step_tokens, C_Q),
                            lambda b, i: (b, jnp.minimum(i + 1, nsteps - 1), col))

    x = qkv.reshape(batch, seq, C_W)
    out = pl.pallas_call(
        functools.partial(_attn_c_kernel, rows=rows),
        grid=(batch, nsteps),
        in_specs=[cur(0), prev(1), cur(1), nxt(1), prev(2), cur(2), nxt(2),
                  _const_spec(bias.shape)],
        out_specs=cur(0),
        out_shape=jax.ShapeDtypeStruct((batch, seq, C_Q), BF16),
        scratch_shapes=[pltpu.VMEM((3 * step_tokens, C_Q), BF16),
                        pltpu.VMEM((3 * step_tokens, C_Q), BF16)],
        compiler_params=_params(2),
        name="attn_neighbourhood",
    )(x, x, x, x, x, x, x, bias)
    return out.reshape(batch * seq, C_Q)


def _c_bias_table(rpb):
    c = jnp.arange(GRID_W)
    coff = jnp.clip(c[None, :] - c[:, None] + (NA_COLS - 1), 0, 2 * NA_COLS - 2)
    t = rpb[:, :, coff]
    tiles = [t[:, st:st + NA_ROWS].transpose(0, 2, 1, 3).reshape(C_HEADS, GRID_W, NA_ROWS * GRID_W)
             for st in range(NA_ROWS)]
    return jnp.stack(tiles).astype(F32)


def _merge_kernel(x_ref, oa_ref, o0_ref, l0_ref, o1_ref, l1_ref, o2_ref, l2_ref, oc_ref,
                  g_ref, wg_ref, wpa_ref, wpb_ref, wpc_ref, wo_ref, out_ref, mix_ref):
    x = x_ref[...]
    hn = _rms(x, g_ref[...]).astype(BF16)
    l0, l1, l2 = l0_ref[...], l1_ref[...], l2_ref[...]
    m = jnp.maximum(jnp.maximum(l0, l1), l2)
    e0, e1, e2 = jnp.exp(l0 - m), jnp.exp(l1 - m), jnp.exp(l2 - m)
    inv = 1.0 / (e0 + e1 + e2)
    ob = ((e0 * inv) * o0_ref[...] + (e1 * inv) * o1_ref[...] + (e2 * inv) * o2_ref[...]).astype(BF16)
    oa, oc = oa_ref[...], oc_ref[...]
    chunk = D_MODEL // 2
    for n in range(D_MODEL // chunk):
        cols = slice(n * chunk, (n + 1) * chunk)
        mix = None
        for k, (o, w_ref) in enumerate(((oa, wpa_ref), (ob, wpb_ref), (oc, wpc_ref))):
            y = jnp.dot(o, w_ref[:, cols], preferred_element_type=F32)
            gcols = slice(k * D_MODEL + n * chunk, k * D_MODEL + (n + 1) * chunk)
            gate = _sigmoid(jnp.dot(hn, wg_ref[:, gcols], preferred_element_type=F32))
            mix = gate * y if mix is None else mix + gate * y
        mix_ref[:, cols] = mix.astype(BF16)
    out_ref[...] = x + jnp.dot(mix_ref[...], wo_ref[...], preferred_element_type=F32)


def _merge(x, oa, ob_parts, oc, g, wg, wpa, wpb, wpc, wo):
    tokens = x.shape[0]
    tm = TOKEN_TILE

    def tok(width):
        return pl.BlockSpec((tm, width), lambda i: (i, 0))

    return pl.pallas_call(
        _merge_kernel,
        grid=(tokens // tm,),
        in_specs=[tok(D_MODEL), tok(A_Q)] + [tok(B_G)] * 6 + [tok(C_Q),
                  _const_spec((1, D_MODEL)), _const_spec((D_MODEL, GATE_W)),
                  _const_spec((A_Q, D_MODEL)), _const_spec((B_G, D_MODEL)),
                  _const_spec((C_Q, D_MODEL)), _const_spec((D_MODEL, D_MODEL))],
        out_specs=tok(D_MODEL),
        out_shape=jax.ShapeDtypeStruct((tokens, D_MODEL), F32),
        scratch_shapes=[pltpu.VMEM((tm, D_MODEL), BF16)],
        compiler_params=_params(1),
        name="merge",
    )(x, oa, *ob_parts, oc, g, wg, wpa, wpb, wpc, wo)


def _mlp_kernel(x_ref, g_ref, wup_ref, wdown_ref, gf_ref, out_ref, *, final):
    x = x_ref[...]
    hn = _rms(x, g_ref[...]).astype(BF16)
    chunk = D_MODEL
    acc = x
    for n in range(D_FF // chunk):
        h = jnp.dot(hn, wup_ref[:, n * chunk:(n + 1) * chunk], preferred_element_type=F32)
        h = jnp.square(jnp.maximum(h, 0.0)).astype(BF16)
        acc = acc + jnp.dot(h, wdown_ref[n * chunk:(n + 1) * chunk, :], preferred_element_type=F32)
    out_ref[...] = _rms(acc, gf_ref[...]) if final else acc


def _mlp(x, g, wup, wdown, gf, final):
    tokens = x.shape[0]
    tm = TOKEN_TILE
    return pl.pallas_call(
        functools.partial(_mlp_kernel, final=final),
        grid=(tokens // tm,),
        in_specs=[pl.BlockSpec((tm, D_MODEL), lambda i: (i, 0)),
                  _const_spec((1, D_MODEL)), _const_spec((D_MODEL, D_FF)),
                  _const_spec((D_FF, D_MODEL)), _const_spec((1, D_MODEL))],
        out_specs=pl.BlockSpec((tm, D_MODEL), lambda i: (i, 0)),
        out_shape=jax.ShapeDtypeStruct((tokens, D_MODEL), F32),
        compiler_params=_params(1),
        name="mlp_final" if final else "mlp",
    )(x, g, wup, wdown, gf)


def _rope_tables(seq):
    half = ROPE_DIM // 2
    inv = ROPE_THETA ** (-jnp.arange(0, ROPE_DIM, 2, dtype=F32) / ROPE_DIM)
    ang = jnp.arange(seq, dtype=F32)[:, None] * inv[None, :]
    cos, sin = jnp.cos(ang), jnp.sin(ang)
    pad = HEAD_DIM - ROPE_DIM
    ones = jnp.ones((seq, pad), F32)
    zeros = jnp.zeros((seq, pad), F32)
    zh = jnp.zeros((seq, half), F32)
    rc = jnp.concatenate([cos, cos, ones], axis=1)
    rsa = jnp.concatenate([-sin, zh, zeros], axis=1)
    rsb = jnp.concatenate([zh, sin, zeros], axis=1)
    return tuple(jnp.tile(t, (1, LANES // HEAD_DIM)) for t in (rc, rsa, rsb))


def _qkv_columns():
    b_q = A_W
    b_span = len(B_GROUPS) * B_G
    cols = list(range(A_W))
    for g in range(len(B_GROUPS)):
        for part in range(3):
            start = b_q + part * b_span + g * B_G
            cols.extend(range(start, start + B_G))
    cols.extend(range(b_q + 3 * b_span, QKV_W))
    return jnp.asarray(cols, dtype=jnp.int32)


def _trunk(x, weights, batch, seq):
    (g1, w_qkv, w_gate, sink_lanes, bias_c, w_pa, w_pb, w_pc, w_o, g2, w_up, w_down, gf) = weights
    rope = _rope_tables(seq)
    h = x.reshape(batch * seq, D_MODEL)
    for layer in range(DEPTH):
        qkv_a, qkv_b, qkv_c = _qkv_proj(h, g1[layer], w_qkv[layer], rope, seq)
        oa = _attn_a(qkv_a, sink_lanes[layer], batch, seq)
        ob_parts = []
        for group in range(len(B_GROUPS)):
            ob_parts.extend(_attn_b_group(qkv_b, group, batch, seq))
        oc = _attn_c(qkv_c, bias_c[layer], batch, seq)
        h = _merge(h, oa, ob_parts, oc, g1[layer], w_gate[layer], w_pa[layer], w_pb[layer],
                   w_pc[layer], w_o[layer])
        h = _mlp(h, g2[layer], w_up[layer], w_down[layer], gf, layer == DEPTH - 1)
    return h.reshape(batch, seq, D_MODEL)


def _prepare_weights(norm1_g, w_in, sink_a, rpb_c, w_pa, w_pb, w_pc, w_o, norm2_g, w_up, w_down, final_g):
    w_qkv = jnp.take(w_in[:, :, :QKV_W], _qkv_columns(), axis=2).astype(BF16)
    w_gate = w_in[:, :, QKV_W:].astype(BF16)
    sink_lanes = jnp.repeat(sink_a.astype(F32), HEAD_DIM, axis=1)[:, None, :]
    bias_c = jax.vmap(_c_bias_table)(rpb_c)
    return (norm1_g.astype(F32)[:, None, :], w_qkv, w_gate, sink_lanes, bias_c,
            w_pa.astype(BF16), w_pb.astype(BF16), w_pc.astype(BF16), w_o.astype(BF16),
            norm2_g.astype(F32)[:, None, :], w_up.astype(BF16), w_down.astype(BF16),
            final_g.astype(F32)[None, :])


def kernel(x_prompt, x_sample, norm1_g, w_in, sink_a, rpb_c, w_pa, w_pb, w_pc, w_o, norm2_g, w_up, w_down,
           final_g):
    weights = _prepare_weights(norm1_g, w_in, sink_a, rpb_c, w_pa, w_pb, w_pc, w_o, norm2_g, w_up,
                               w_down, final_g)
    y_prompt = _trunk(x_prompt, weights, *x_prompt.shape[:2])
    y_sample = _trunk(x_sample, weights, *x_sample.shape[:2])
    return (y_prompt, y_sample)
```

```python
import functools

import jax
import jax.numpy as jnp
from jax import lax
from jax.experimental import pallas as pl
from jax.experimental.pallas import tpu as pltpu

D_MODEL = 1024
DEPTH = 4
HEAD_DIM = 64
A_Q_HEADS = 6
A_KV_HEADS = 2
A_HALF_WINDOW = 128
B_GROUPS = ((128, 1), (512, 4), (2048, 16))
B_HEADS_PER_GROUP = 4
C_HEADS = 6
GRID_W = 64
NA_ROWS = 8
NA_COLS = 16
ROPE_THETA = 500000.0
ROPE_DIM = HEAD_DIM // 4
D_FF = 4 * D_MODEL
NORM_EPS = 1e-6
N_BRANCHES = 3
NEG = -1e30

A_Q = A_Q_HEADS * HEAD_DIM
A_KV = A_KV_HEADS * HEAD_DIM
A_W = A_Q + 2 * A_KV
B_G = B_HEADS_PER_GROUP * HEAD_DIM
B_GW = 3 * B_G
B_W = len(B_GROUPS) * B_GW
C_Q = C_HEADS * HEAD_DIM
C_W = 3 * C_Q
QKV_W = A_W + B_W + C_W
GATE_W = N_BRANCHES * D_MODEL
Q_SCALE = HEAD_DIM ** -0.5

LANES = 128
HALF = LANES // 2
VMEM_LIMIT_BYTES = 56 * 1024 * 1024
TOKEN_TILE = 512
QKV_CHUNK = 512
C_ROWS_PER_STEP = 8
C_HALO_ROWS = NA_ROWS // 2

F32 = jnp.float32
BF16 = jnp.bfloat16


def _params(n_axes):
    return pltpu.CompilerParams(dimension_semantics=("arbitrary",) * n_axes,
                                vmem_limit_bytes=VMEM_LIMIT_BYTES)


def _const_spec(shape):
    zeros = (0,) * len(shape)
    return pl.BlockSpec(shape, lambda *_: zeros)


def _rms(x, g):
    return x * lax.rsqrt(jnp.mean(x * x, axis=-1, keepdims=True) + NORM_EPS) * g


def _sigmoid(z):
    return 1.0 / (1.0 + jnp.exp(-z))


def _nt_dot(a, b):
    return lax.dot_general(a, b, (((1,), (1,)), ((), ())), preferred_element_type=F32)


def _qkv_plan():
    plan = []
    for n in range(A_W // LANES):
        kind = 0 if n < A_Q // LANES else (1 if n < (A_Q + A_KV) // LANES else 2)
        plan.append((0, n * LANES, kind < 2, kind == 0))
    for n in range(B_W // LANES):
        group, off = divmod(n * LANES, B_GW)
        kind = off // B_G
        plan.append((1 + group, off, kind < 2, kind == 0))
    for n in range(C_W // LANES):
        plan.append((1 + len(B_GROUPS), n * LANES, False, n < C_Q // LANES))
    return plan


def _qkv_kernel(x_ref, g_ref, w_ref, rc_ref, rsa_ref, rsb_ref, a_ref, b0_ref, b1_ref, b2_ref, c_ref,
                stage_ref):
    tm = x_ref.shape[0]
    hn = _rms(x_ref[...], g_ref[...]).astype(BF16)
    rc, rsa, rsb = rc_ref[...], rsa_ref[...], rsb_ref[...]
    outs = (a_ref, b0_ref, b1_ref, b2_ref, c_ref)
    dils = (1,) + tuple(d for _, d in B_GROUPS) + (1,)
    plan = _qkv_plan()
    per_chunk = QKV_CHUNK // LANES
    staged = 0
    for chunk in range(QKV_W // QKV_CHUNK):
        acc = jnp.dot(hn, w_ref[:, chunk * QKV_CHUNK:(chunk + 1) * QKV_CHUNK],
                      preferred_element_type=F32)
        for sub in range(per_chunk):
            dest, off, rope, scale = plan[chunk * per_chunk + sub]
            t = acc[:, sub * LANES:(sub + 1) * LANES]
            if rope:
                t = (t * rc + pltpu.roll(t, LANES - ROPE_DIM // 2, 1) * rsa
                     + pltpu.roll(t, ROPE_DIM // 2, 1) * rsb)
            if scale:
                t = t * Q_SCALE
            dil = dils[dest]
            if dil == 1:
                outs[dest][:, off:off + LANES] = t.astype(BF16)
            else:
                stage_ref[staged] = t
                for r in range(dil):
                    rows = stage_ref[staged, pl.ds(r, tm // dil, stride=dil), :]
                    outs[dest][r, :, off:off + LANES] = rows.astype(BF16)
                staged += 1


def _qkv_proj(x, g, w, rope, batch, seq):
    tokens = x.shape[0]
    tm = TOKEN_TILE
    pos_blocks = seq // tm
    rope_spec = pl.BlockSpec((tm, LANES), lambda i: (i % pos_blocks, 0))
    out_specs = [pl.BlockSpec((tm, A_W), lambda i: (i, 0))]
    out_shape = [jax.ShapeDtypeStruct((tokens, A_W), BF16)]
    n_staged = 0
    for _, dil in B_GROUPS:
        if dil == 1:
            out_specs.append(pl.BlockSpec((tm, B_GW), lambda i: (i, 0)))
            out_shape.append(jax.ShapeDtypeStruct((tokens, B_GW), BF16))
        else:
            out_specs.append(pl.BlockSpec((None, dil, tm // dil, B_GW),
                                          lambda i: (i // pos_blocks, 0, i % pos_blocks, 0)))
            out_shape.append(jax.ShapeDtypeStruct((batch, dil, seq // dil, B_GW), BF16))
            n_staged += B_GW // LANES
    out_specs.append(pl.BlockSpec((tm, C_W), lambda i: (i, 0)))
    out_shape.append(jax.ShapeDtypeStruct((tokens, C_W), BF16))
    return pl.pallas_call(
        _qkv_kernel,
        grid=(tokens // tm,),
        in_specs=[pl.BlockSpec((tm, D_MODEL), lambda i: (i, 0)),
                  _const_spec((1, D_MODEL)),
                  _const_spec((D_MODEL, QKV_W)),
                  rope_spec, rope_spec, rope_spec],
        out_specs=out_specs,
        out_shape=out_shape,
        scratch_shapes=[pltpu.VMEM((n_staged, tm, LANES), F32)],
        compiler_params=_params(1),
        name="qkv_proj",
    )(x, g, w, *rope)


def _lane_lo(shape):
    return lax.broadcasted_iota(jnp.int32, shape, len(shape) - 1) < HALF


def _pair_scores(qc, keys):
    lo = _lane_lo(qc.shape)
    zero = jnp.zeros_like(qc)
    stacked = jnp.concatenate([jnp.where(lo, qc, zero), jnp.where(lo, zero, qc)], axis=0)
    return _nt_dot(stacked, keys)


def _attend(n_tasks, load, emit):
    def scores(t):
        qc, keys, vals, valid, bias = load(t)
        sc = _pair_scores(qc, keys)
        if bias is not None:
            sc = sc + bias
        return jnp.where(valid, sc, NEG), vals

    ahead = scores(0)
    for t in range(n_tasks):
        sc, vals = ahead
        if t + 1 < n_tasks:
            ahead = scores(t + 1)
        m = jnp.max(sc, axis=-1, keepdims=True)
        e = jnp.exp(sc - m)
        den = jnp.sum(e, axis=-1, keepdims=True)
        pn = (e * (1.0 / den)).astype(BF16)
        lse = m + jnp.log(den)
        o2 = jnp.dot(pn, vals, preferred_element_type=F32)
        rows = o2.shape[0] // 2
        lo = _lane_lo((rows, LANES))
        emit(t, jnp.where(lo, o2[:rows], o2[rows:]), jnp.where(lo, lse[:rows], lse[rows:]))


def _band_valid(rows, keys, halo, key0, length):
    row = lax.broadcasted_iota(jnp.int32, (2 * rows, keys), 0) % rows
    col = lax.broadcasted_iota(jnp.int32, (2 * rows, keys), 1)
    kpos = key0 + col
    return (jnp.abs(col - halo - row) <= halo) & (kpos >= 0) & (kpos < length)


def _attn_a_kernel(q_ref, kl_ref, km_ref, kr_ref, vl_ref, vm_ref, vr_ref, sink_ref, o_ref,
                   *, tq, seq):
    i = pl.program_id(1)
    blk = A_HALF_WINDOW
    n_chunks = A_Q // LANES

    def kv_variants(left, mid, right):
        cat = jnp.concatenate([left[...], mid[...], right[...]], axis=0)
        swapped = pltpu.roll(cat.astype(F32), HALF, 1).astype(cat.dtype)
        lo = _lane_lo(cat.shape)
        return (jnp.where(lo, cat, swapped), cat, jnp.where(lo, swapped, cat))

    kvar = kv_variants(kl_ref, km_ref, kr_ref)
    vvar = kv_variants(vl_ref, vm_ref, vr_ref)

    def load(t):
        j, p = divmod(t, n_chunks)
        valid = _band_valid(blk, 3 * blk, blk, i * tq + (j - 1) * blk, seq)
        win = slice(j * blk, (j + 3) * blk)
        return (q_ref[j * blk:(j + 1) * blk, p * LANES:(p + 1) * LANES], kvar[p][win], vvar[p][win],
                valid, None)

    def emit(t, o, lse):
        j, p = divmod(t, n_chunks)
        cols = slice(p * LANES, (p + 1) * LANES)
        o_ref[j * blk:(j + 1) * blk, cols] = (o * _sigmoid(lse - sink_ref[:, cols])).astype(o_ref.dtype)

    _attend((tq // blk) * n_chunks, load, emit)


def _attn_a(qkv, sink_lanes, batch, seq):
    blk = A_HALF_WINDOW
    tq = 2 * blk
    per = tq // blk
    nblk = seq // blk
    kcol, vcol = A_Q // A_KV, A_Q // A_KV + 1

    def main(col):
        return pl.BlockSpec((None, tq, A_KV), lambda b, i: (b, i, col))

    def left(col):
        return pl.BlockSpec((None, blk, A_KV), lambda b, i: (b, jnp.maximum(i * per - 1, 0), col))

    def right(col):
        return pl.BlockSpec((None, blk, A_KV),
                            lambda b, i: (b, jnp.minimum((i + 1) * per, nblk - 1), col))

    x = qkv.reshape(batch, seq, A_W)
    out = pl.pallas_call(
        functools.partial(_attn_a_kernel, tq=tq, seq=seq),
        grid=(batch, seq // tq),
        in_specs=[pl.BlockSpec((None, tq, A_Q), lambda b, i: (b, i, 0)),
                  left(kcol), main(kcol), right(kcol),
                  left(vcol), main(vcol), right(vcol),
                  _const_spec((1, A_Q))],
        out_specs=pl.BlockSpec((None, tq, A_Q), lambda b, i: (b, i, 0)),
        out_shape=jax.ShapeDtypeStruct((batch, seq, A_Q), BF16),
        compiler_params=_params(2),
        name="attn_window",
    )(x, x, x, x, x, x, x, sink_lanes)
    return out.reshape(batch * seq, A_Q)


def _attn_b_kernel(q_ref, kl_ref, km_ref, kr_ref, vl_ref, vm_ref, vr_ref, o_ref, lse_ref,
                   *, tq, sub, halo, dil):
    i = pl.program_id(1)
    r = pl.program_id(2)
    blk = 2 * halo
    n_chunks = B_G // LANES
    kcat = jnp.concatenate([kl_ref[...], km_ref[...], kr_ref[...]], axis=0)
    vcat = jnp.concatenate([vl_ref[...], vm_ref[...], vr_ref[...]], axis=0)

    def load(t):
        j, p = divmod(t, n_chunks)
        valid = _band_valid(blk, 2 * blk, halo, i * tq + j * blk - halo, sub)
        win = slice(j * blk, (j + 2) * blk)
        cols = slice(p * LANES, (p + 1) * LANES)
        return q_ref[j * blk:(j + 1) * blk, cols], kcat[win, cols], vcat[win, cols], valid, None

    def emit(t, o, lse):
        j, p = divmod(t, n_chunks)
        if dil == 1:
            rows = slice(j * blk, (j + 1) * blk)
        else:
            rows = pl.ds(j * blk * dil + r, blk, stride=dil)
        o_ref[p, rows, :] = o
        lse_ref[p, rows, :] = lse

    _attend((tq // blk) * n_chunks, load, emit)


def _attn_b_group(qkv, group, batch, seq):
    window, dil = B_GROUPS[group]
    halo = window // (2 * dil)
    sub = seq // dil
    tq = min(4 * halo if dil > 4 else 8 * halo, sub)
    per = tq // halo
    nhalo = sub // halo

    def main(col):
        return pl.BlockSpec((None, None, tq, B_G), lambda b, i, r: (b, r, i, col))

    def left(col):
        return pl.BlockSpec((None, None, halo, B_G),
                            lambda b, i, r: (b, r, jnp.maximum(i * per - 1, 0), col))

    def right(col):
        return pl.BlockSpec((None, None, halo, B_G),
                            lambda b, i, r: (b, r, jnp.minimum((i + 1) * per, nhalo - 1), col))

    x = qkv.reshape(batch, dil, sub, B_GW)
    n_chunks = B_G // LANES
    out_spec = pl.BlockSpec((None, n_chunks, tq * dil, LANES), lambda b, i, r: (b, 0, i, 0))
    out_shape = jax.ShapeDtypeStruct((batch, n_chunks, seq, LANES), F32)
    o, lse = pl.pallas_call(
        functools.partial(_attn_b_kernel, tq=tq, sub=sub, halo=halo, dil=dil),
        grid=(batch, sub // tq, dil),
        in_specs=[main(0), left(1), main(1), right(1), left(2), main(2), right(2)],
        out_specs=[out_spec, out_spec],
        out_shape=[out_shape, out_shape],
        compiler_params=_params(3),
        name=f"attn_dilated_{dil}",
    )(x, x, x, x, x, x, x)
    return o, lse


def _attn_c_kernel(q_ref, kp_ref, kc_ref, kn_ref, vp_ref, vc_ref, vn_ref, bias_ref, o_ref,
                   kcat, vcat, *, rows):
    i = pl.program_id(1)
    halo_tokens = C_HALO_ROWS * GRID_W
    step_tokens = C_ROWS_PER_STEP * GRID_W
    win = NA_ROWS * GRID_W
    n_chunks = C_Q // LANES
    for cat, (p_ref, c_ref, n_ref) in ((kcat, (kp_ref, kc_ref, kn_ref)), (vcat, (vp_ref, vc_ref, vn_ref))):
        cat[0:halo_tokens] = p_ref[...]
        cat[halo_tokens:halo_tokens + step_tokens] = c_ref[...]
        cat[halo_tokens + step_tokens:] = n_ref[...]
    qcol = lax.broadcasted_iota(jnp.int32, (2 * GRID_W, win), 0) % GRID_W
    kcol = lax.broadcasted_iota(jnp.int32, (2 * GRID_W, win), 1) % GRID_W
    cstart = jnp.clip(qcol - NA_COLS // 2, 0, GRID_W - NA_COLS)
    valid = (kcol >= cstart) & (kcol < cstart + NA_COLS)

    def load(t):
        rr, p = divmod(t, n_chunks)
        r = i * C_ROWS_PER_STEP + rr
        rstart = jnp.clip(r - NA_ROWS // 2, 0, rows - NA_ROWS)
        start = rstart - r + (NA_ROWS - 1)
        off = pl.multiple_of((rstart - i * C_ROWS_PER_STEP + C_HALO_ROWS) * GRID_W, GRID_W)
        cols = slice(p * LANES, (p + 1) * LANES)
        return (q_ref[rr * GRID_W:(rr + 1) * GRID_W, cols], kcat[pl.ds(off, win), cols],
                vcat[pl.ds(off, win), cols], valid, bias_ref[start, p])

    def emit(t, o, lse):
        rr, p = divmod(t, n_chunks)
        o_ref[rr * GRID_W:(rr + 1) * GRID_W, p * LANES:(p + 1) * LANES] = o.astype(o_ref.dtype)

    _attend(C_ROWS_PER_STEP * n_chunks, load, emit)


def _attn_c(qkv, bias, batch, seq):
    rows = seq // GRID_W
    assert rows >= NA_ROWS and rows % C_ROWS_PER_STEP == 0
    step_tokens = C_ROWS_PER_STEP * GRID_W
    halo_tokens = C_HALO_ROWS * GRID_W
    per = step_tokens // halo_tokens
    nsteps = rows // C_ROWS_PER_STEP
    nhalo = seq // halo_tokens

    def prev(col):
        return pl.BlockSpec((None, halo_tokens, C_Q),
                            lambda b, i: (b, jnp.maximum(i * per - 1, 0), col))

    def cur(col):
        return pl.BlockSpec((None, step_tokens, C_Q), lambda b, i: (b, i, col))

    def nxt(col):
        return pl.BlockSpec((None, halo_tokens, C_Q),
                            lambda b, i: (b, jnp.minimum((i + 1) * per, nhalo - 1), col))

    x = qkv.reshape(batch, seq, C_W)
    cat_tokens = step_tokens + 2 * halo_tokens
    out = pl.pallas_call(
        functools.partial(_attn_c_kernel, rows=rows),
        grid=(batch, nsteps),
        in_specs=[cur(0), prev(1), cur(1), nxt(1), prev(2), cur(2), nxt(2),
                  _const_spec(bias.shape)],
        out_specs=cur(0),
        out_shape=jax.ShapeDtypeStruct((batch, seq, C_Q), BF16),
        scratch_shapes=[pltpu.VMEM((cat_tokens, C_Q), BF16), pltpu.VMEM((cat_tokens, C_Q), BF16)],
        compiler_params=_params(2),
        name="attn_neighbourhood",
    )(x, x, x, x, x, x, x, bias)
    return out.reshape(batch * seq, C_Q)


def _c_bias_table(rpb):
    c = jnp.arange(GRID_W)
    coff = jnp.clip(c[None, :] - c[:, None] + (NA_COLS - 1), 0, 2 * NA_COLS - 2)
    t = rpb[:, :, coff]
    tiles = [t[:, st:st + NA_ROWS].transpose(0, 2, 1, 3).reshape(C_HEADS // 2, 2 * GRID_W, NA_ROWS * GRID_W)
             for st in range(NA_ROWS)]
    return jnp.stack(tiles).astype(F32)


def _merge_kernel(x_ref, oa_ref, o0_ref, l0_ref, o1_ref, l1_ref, o2_ref, l2_ref, oc_ref,
                  g_ref, wg_ref, wpa_ref, wpb_ref, wpc_ref, wo_ref, out_ref, mix_ref):
    x = x_ref[...]
    hn = _rms(x, g_ref[...]).astype(BF16)
    ob = []
    for p in range(B_G // LANES):
        l0, l1, l2 = l0_ref[p], l1_ref[p], l2_ref[p]
        m = jnp.maximum(jnp.maximum(l0, l1), l2)
        e0, e1, e2 = jnp.exp(l0 - m), jnp.exp(l1 - m), jnp.exp(l2 - m)
        inv = 1.0 / (e0 + e1 + e2)
        ob.append(((e0 * inv) * o0_ref[p] + (e1 * inv) * o1_ref[p] + (e2 * inv) * o2_ref[p]).astype(BF16))
    ob = jnp.concatenate(ob, axis=1)
    oa, oc = oa_ref[...], oc_ref[...]
    chunk = D_MODEL // 2
    for n in range(D_MODEL // chunk):
        cols = slice(n * chunk, (n + 1) * chunk)
        mix = None
        for k, (o, w_ref) in enumerate(((oa, wpa_ref), (ob, wpb_ref), (oc, wpc_ref))):
            y = jnp.dot(o, w_ref[:, cols], preferred_element_type=F32)
            gcols = slice(k * D_MODEL + n * chunk, k * D_MODEL + (n + 1) * chunk)
            gate = _sigmoid(jnp.dot(hn, wg_ref[:, gcols], preferred_element_type=F32))
            mix = gate * y if mix is None else mix + gate * y
        mix_ref[:, cols] = mix.astype(BF16)
    out_ref[...] = x + jnp.dot(mix_ref[...], wo_ref[...], preferred_element_type=F32)


def _merge(x, oa, ob_parts, oc, g, wg, wpa, wpb, wpc, wo, seq):
    tokens = x.shape[0]
    tm = TOKEN_TILE
    pos_blocks = seq // tm

    def tok(width):
        return pl.BlockSpec((tm, width), lambda i: (i, 0))

    b_spec = pl.BlockSpec((None, B_G // LANES, tm, LANES),
                          lambda i: (i // pos_blocks, 0, i % pos_blocks, 0))
    return pl.pallas_call(
        _merge_kernel,
        grid=(tokens // tm,),
        in_specs=[tok(D_MODEL), tok(A_Q)] + [b_spec] * 6 + [tok(C_Q),
                  _const_spec((1, D_MODEL)), _const_spec((D_MODEL, GATE_W)),
                  _const_spec((A_Q, D_MODEL)), _const_spec((B_G, D_MODEL)),
                  _const_spec((C_Q, D_MODEL)), _const_spec((D_MODEL, D_MODEL))],
        out_specs=tok(D_MODEL),
        out_shape=jax.ShapeDtypeStruct((tokens, D_MODEL), F32),
        scratch_shapes=[pltpu.VMEM((tm, D_MODEL), BF16)],
        compiler_params=_params(1),
        name="merge",
    )(x, oa, *ob_parts, oc, g, wg, wpa, wpb, wpc, wo)


def _mlp_kernel(x_ref, g_ref, wup_ref, wdown_ref, gf_ref, out_ref, *, final):
    x = x_ref[...]
    hn = _rms(x, g_ref[...]).astype(BF16)
    chunk = D_MODEL
    acc = x
    for n in range(D_FF // chunk):
        h = jnp.dot(hn, wup_ref[:, n * chunk:(n + 1) * chunk], preferred_element_type=F32)
        h = jnp.square(jnp.maximum(h, 0.0)).astype(BF16)
        acc = acc + jnp.dot(h, wdown_ref[n * chunk:(n + 1) * chunk, :], preferred_element_type=F32)
    out_ref[...] = _rms(acc, gf_ref[...]) if final else acc


def _mlp(x, g, wup, wdown, gf, final):
    tokens = x.shape[0]
    tm = TOKEN_TILE
    return pl.pallas_call(
        functools.partial(_mlp_kernel, final=final),
        grid=(tokens // tm,),
        in_specs=[pl.BlockSpec((tm, D_MODEL), lambda i: (i, 0)),
                  _const_spec((1, D_MODEL)), _const_spec((D_MODEL, D_FF)),
                  _const_spec((D_FF, D_MODEL)), _const_spec((1, D_MODEL))],
        out_specs=pl.BlockSpec((tm, D_MODEL), lambda i: (i, 0)),
        out_shape=jax.ShapeDtypeStruct((tokens, D_MODEL), F32),
        compiler_params=_params(1),
        name="mlp_final" if final else "mlp",
    )(x, g, wup, wdown, gf)


def _rope_tables(seq):
    half = ROPE_DIM // 2
    inv = ROPE_THETA ** (-jnp.arange(0, ROPE_DIM, 2, dtype=F32) / ROPE_DIM)
    ang = jnp.arange(seq, dtype=F32)[:, None] * inv[None, :]
    cos, sin = jnp.cos(ang), jnp.sin(ang)
    pad = HEAD_DIM - ROPE_DIM
    ones = jnp.ones((seq, pad), F32)
    zeros = jnp.zeros((seq, pad), F32)
    zh = jnp.zeros((seq, half), F32)
    rc = jnp.concatenate([cos, cos, ones], axis=1)
    rsa = jnp.concatenate([-sin, zh, zeros], axis=1)
    rsb = jnp.concatenate([zh, sin, zeros], axis=1)
    return tuple(jnp.tile(t, (1, LANES // HEAD_DIM)) for t in (rc, rsa, rsb))


def _qkv_columns():
    b_q = A_W
    b_span = len(B_GROUPS) * B_G
    cols = list(range(A_W))
    for g in range(len(B_GROUPS)):
        for part in range(3):
            start = b_q + part * b_span + g * B_G
            cols.extend(range(start, start + B_G))
    cols.extend(range(b_q + 3 * b_span, QKV_W))
    return jnp.asarray(cols, dtype=jnp.int32)


def _trunk(x, weights, batch, seq):
    (g1, w_qkv, w_gate, sink_lanes, bias_c, w_pa, w_pb, w_pc, w_o, g2, w_up, w_down, gf) = weights
    rope = _rope_tables(seq)
    h = x.reshape(batch * seq, D_MODEL)
    for layer in range(DEPTH):
        qkv = _qkv_proj(h, g1[layer], w_qkv[layer], rope, batch, seq)
        qkv_a, qkv_b, qkv_c = qkv[0], qkv[1:-1], qkv[-1]
        oa = _attn_a(qkv_a, sink_lanes[layer], batch, seq)
        ob_parts = []
        for group in range(len(B_GROUPS)):
            ob_parts.extend(_attn_b_group(qkv_b[group], group, batch, seq))
        oc = _attn_c(qkv_c, bias_c[layer], batch, seq)
        h = _merge(h, oa, ob_parts, oc, g1[layer], w_gate[layer], w_pa[layer], w_pb[layer],
                   w_pc[layer], w_o[layer], seq)
        h = _mlp(h, g2[layer], w_up[layer], w_down[layer], gf, layer == DEPTH - 1)
    return h.reshape(batch, seq, D_MODEL)


def _prepare_weights(norm1_g, w_in, sink_a, rpb_c, w_pa, w_pb, w_pc, w_o, norm2_g, w_up, w_down, final_g):
    w_qkv = jnp.take(w_in[:, :, :QKV_W], _qkv_columns(), axis=2).astype(BF16)
    w_gate = w_in[:, :, QKV_W:].astype(BF16)
    sink_lanes = jnp.repeat(sink_a.astype(F32), HEAD_DIM, axis=1)[:, None, :]
    bias_c = jax.vmap(_c_bias_table)(rpb_c)
    return (norm1_g.astype(F32)[:, None, :], w_qkv, w_gate, sink_lanes, bias_c,
            w_pa.astype(BF16), w_pb.astype(BF16), w_pc.astype(BF16), w_o.astype(BF16),
            norm2_g.astype(F32)[:, None, :], w_up.astype(BF16), w_down.astype(BF16),
            final_g.astype(F32)[None, :])


def kernel(x_prompt, x_sample, norm1_g, w_in, sink_a, rpb_c, w_pa, w_pb, w_pc, w_o, norm2_g, w_up, w_down,
           final_g):
    weights = _prepare_weights(norm1_g, w_in, sink_a, rpb_c, w_pa, w_pb, w_pc, w_o, norm2_g, w_up,
                               w_down, final_g)
    y_prompt = _trunk(x_prompt, weights, *x_prompt.shape[:2])
    y_sample = _trunk(x_sample, weights, *x_sample.shape[:2])
    return (y_prompt, y_sample)
```

```python
import collections
import functools

import jax
import jax.numpy as jnp
from jax import lax
from jax.experimental import pallas as pl
from jax.experimental.pallas import tpu as pltpu

D_MODEL = 1024
DEPTH = 4
HEAD_DIM = 64
A_Q_HEADS = 6
A_KV_HEADS = 2
A_HALF_WINDOW = 128
B_GROUPS = ((128, 1), (512, 4), (2048, 16))
B_HEADS_PER_GROUP = 4
C_HEADS = 6
GRID_W = 64
NA_ROWS = 8
NA_COLS = 16
ROPE_THETA = 500000.0
ROPE_DIM = HEAD_DIM // 4
D_FF = 4 * D_MODEL
NORM_EPS = 1e-6
N_BRANCHES = 3
NEG = -1e30

A_Q = A_Q_HEADS * HEAD_DIM
A_KV = A_KV_HEADS * HEAD_DIM
A_W = A_Q + 2 * A_KV
B_G = B_HEADS_PER_GROUP * HEAD_DIM
B_GW = 3 * B_G
B_W = len(B_GROUPS) * B_GW
C_Q = C_HEADS * HEAD_DIM
C_W = 3 * C_Q
QKV_W = A_W + B_W + C_W
GATE_W = N_BRANCHES * D_MODEL
Q_SCALE = HEAD_DIM ** -0.5

LANES = 128
HALF = LANES // 2
VMEM_LIMIT_BYTES = 56 * 1024 * 1024
TOKEN_TILE = 1024
QKV_CHUNK = 512
MERGE_CHUNK = 512
C_ROWS_PER_STEP = 8
C_HALO_ROWS = NA_ROWS // 2
ATTN_AHEAD = 2
A_BLOCKS_PER_STEP = 4
A_TASK_ROWS = 128

F32 = jnp.float32
BF16 = jnp.bfloat16


def _params(n_axes):
    return pltpu.CompilerParams(dimension_semantics=("arbitrary",) * n_axes,
                                vmem_limit_bytes=VMEM_LIMIT_BYTES)


def _const_spec(shape):
    zeros = (0,) * len(shape)
    return pl.BlockSpec(shape, lambda *_: zeros, pipeline_mode=pl.Buffered(1))


def _rms(x, g):
    return x * lax.rsqrt(jnp.mean(x * x, axis=-1, keepdims=True) + NORM_EPS) * g


def _sigmoid(z):
    return 1.0 / (1.0 + jnp.exp(-z))


def _nt_dot(a, b):
    return lax.dot_general(a, b, (((1,), (1,)), ((), ())), preferred_element_type=F32)


def _qkv_plan():
    plan = []
    for n in range(A_W // LANES):
        kind = 0 if n < A_Q // LANES else (1 if n < (A_Q + A_KV) // LANES else 2)
        plan.append((0, n * LANES, kind < 2, kind == 0))
    for n in range(B_W // LANES):
        group, off = divmod(n * LANES, B_GW)
        kind = off // B_G
        plan.append((1 + group, off, kind < 2, kind == 0))
    for n in range(C_W // LANES):
        plan.append((1 + len(B_GROUPS), n * LANES, False, n < C_Q // LANES))
    return plan


def _qkv_kernel(x_ref, g_ref, w_ref, rc_ref, rsa_ref, rsb_ref, a_ref, b0_ref, b1_ref, b2_ref, c_ref,
                stage_ref):
    tm = x_ref.shape[0]
    hn = _rms(x_ref[...], g_ref[...]).astype(BF16)
    rc, rsa, rsb = rc_ref[...], rsa_ref[...], rsb_ref[...]
    outs = (a_ref, b0_ref, b1_ref, b2_ref, c_ref)
    dils = (1,) + tuple(d for _, d in B_GROUPS) + (1,)
    plan = _qkv_plan()
    per_chunk = QKV_CHUNK // LANES
    staged = 0
    for chunk in range(QKV_W // QKV_CHUNK):
        acc = jnp.dot(hn, w_ref[:, chunk * QKV_CHUNK:(chunk + 1) * QKV_CHUNK],
                      preferred_element_type=F32)
        for sub in range(per_chunk):
            dest, off, rope, scale = plan[chunk * per_chunk + sub]
            t = acc[:, sub * LANES:(sub + 1) * LANES]
            if rope:
                t = (t * rc + pltpu.roll(t, LANES - ROPE_DIM // 2, 1) * rsa
                     + pltpu.roll(t, ROPE_DIM // 2, 1) * rsb)
            if scale:
                t = t * Q_SCALE
            dil = dils[dest]
            if dil == 1:
                outs[dest][:, off:off + LANES] = t.astype(BF16)
            else:
                stage_ref[staged] = t
                for r in range(dil):
                    rows = stage_ref[staged, pl.ds(r, tm // dil, stride=dil), :]
                    outs[dest][r, :, off:off + LANES] = rows.astype(BF16)
                staged += 1


def _qkv_proj(x, g, w, rope, batch, seq):
    tokens = x.shape[0]
    tm = TOKEN_TILE
    pos_blocks = seq // tm
    rope_spec = pl.BlockSpec((tm, LANES), lambda i: (i % pos_blocks, 0))
    out_specs = [pl.BlockSpec((tm, A_W), lambda i: (i, 0))]
    out_shape = [jax.ShapeDtypeStruct((tokens, A_W), BF16)]
    n_staged = 0
    for _, dil in B_GROUPS:
        if dil == 1:
            out_specs.append(pl.BlockSpec((tm, B_GW), lambda i: (i, 0)))
            out_shape.append(jax.ShapeDtypeStruct((tokens, B_GW), BF16))
        else:
            out_specs.append(pl.BlockSpec((None, dil, tm // dil, B_GW),
                                          lambda i: (i // pos_blocks, 0, i % pos_blocks, 0)))
            out_shape.append(jax.ShapeDtypeStruct((batch, dil, seq // dil, B_GW), BF16))
            n_staged += B_GW // LANES
    out_specs.append(pl.BlockSpec((tm, C_W), lambda i: (i, 0)))
    out_shape.append(jax.ShapeDtypeStruct((tokens, C_W), BF16))
    return pl.pallas_call(
        _qkv_kernel,
        grid=(tokens // tm,),
        in_specs=[pl.BlockSpec((tm, D_MODEL), lambda i: (i, 0)),
                  _const_spec((1, D_MODEL)),
                  _const_spec((D_MODEL, QKV_W)),
                  rope_spec, rope_spec, rope_spec],
        out_specs=out_specs,
        out_shape=out_shape,
        scratch_shapes=[pltpu.VMEM((n_staged, tm, LANES), F32)],
        compiler_params=_params(1),
        name="qkv_proj",
    )(x, g, w, *rope)


def _lane_lo(shape):
    return lax.broadcasted_iota(jnp.int32, shape, len(shape) - 1) < HALF


def _pair_scores(qc, keys):
    lo = _lane_lo(qc.shape)
    zero = jnp.zeros_like(qc)
    stacked = jnp.concatenate([jnp.where(lo, qc, zero), jnp.where(lo, zero, qc)], axis=0)
    return _nt_dot(stacked, keys)


_Task = collections.namedtuple("_Task", "q keys vals mask mask_cols bias")


def _masked_scores(tk):
    s = _pair_scores(tk.q(), tk.keys())
    if tk.bias is not None:
        s = s + tk.bias()
    pieces, done = [], 0
    for c0, c1 in tk.mask_cols:
        if c0 > done:
            pieces.append(s[:, done:c0])
        keep = tk.mask(c0, c1) != 0.0
        pieces.append(jnp.where(jnp.concatenate([keep, keep], axis=0), s[:, c0:c1], NEG))
        done = c1
    if done < s.shape[1]:
        pieces.append(s[:, done:])
    return pieces[0] if len(pieces) == 1 else jnp.concatenate(pieces, axis=1)


def _attend(n_tasks, task, emit, *, want_lse):
    pending = [_masked_scores(task(t)) for t in range(min(ATTN_AHEAD, n_tasks))]
    for t in range(n_tasks):
        if t + ATTN_AHEAD < n_tasks:
            pending.append(_masked_scores(task(t + ATTN_AHEAD)))
        s = pending.pop(0)
        m = jnp.max(s, axis=-1, keepdims=True)
        e = jnp.exp(s - m)
        den = jnp.sum(e, axis=-1, keepdims=True)
        o2 = jnp.dot(e.astype(BF16), task(t).vals(), preferred_element_type=F32) * (1.0 / den)
        rows = o2.shape[0] // 2
        lo = _lane_lo((rows, LANES))
        lse = None
        if want_lse:
            l2 = m + jnp.log(den)
            lse = jnp.where(lo, l2[:rows], l2[rows:])
        emit(t, jnp.where(lo, o2[:rows], o2[rows:]), lse)


def _band_mask_table(rows, keys, halo):
    row = jnp.arange(rows)[:, None]
    col = jnp.arange(keys)[None, :]
    band = jnp.abs(col - halo - row) <= halo
    variants = [band & ((col >= halo) | (not first)) & ((col < keys - halo) | (not last))
                for last in (False, True) for first in (False, True)]
    return jnp.stack(variants).astype(F32)


def _band_variant(q0, rows, length):
    return jnp.where(q0 == 0, 1, 0) + jnp.where(q0 + rows == length, 2, 0)


def _attn_a_kernel(q_ref, kl_ref, km_ref, kr_ref, vl_ref, vm_ref, vr_ref, sink_ref, mask_ref, o_ref,
                   *, tq, seq):
    i = pl.program_id(1)
    blk = A_HALF_WINDOW
    n_chunks = A_Q // LANES

    def kv_variants(left, mid, right):
        cat = jnp.concatenate([left[...], mid[...], right[...]], axis=0)
        swapped = pltpu.roll(cat.astype(F32), HALF, 1).astype(cat.dtype)
        lo = _lane_lo(cat.shape)
        return (jnp.where(lo, cat, swapped), cat, jnp.where(lo, swapped, cat))

    kvar = kv_variants(kl_ref, km_ref, kr_ref)
    vvar = kv_variants(vl_ref, vm_ref, vr_ref)

    variants = [_band_variant(i * tq + j * blk, blk, seq) for j in range(tq // blk)]

    rows = A_TASK_ROWS
    per_blk = blk // rows

    def task(t):
        jc, p = divmod(t, n_chunks)
        j, c = divmod(jc, per_blk)
        win = slice(j * blk, (j + 3) * blk)
        return _Task(q=lambda: q_ref[jc * rows:(jc + 1) * rows, p * LANES:(p + 1) * LANES],
                     keys=lambda: kvar[p][win], vals=lambda: vvar[p][win],
                     mask=lambda c0, c1: mask_ref[variants[j], c * rows:(c + 1) * rows, c0:c1],
                     mask_cols=((0, blk), (2 * blk, 3 * blk)), bias=None)

    def emit(t, o, lse):
        jc, p = divmod(t, n_chunks)
        cols = slice(p * LANES, (p + 1) * LANES)
        o_ref[jc * rows:(jc + 1) * rows, cols] = (o * _sigmoid(lse - sink_ref[:, cols])).astype(o_ref.dtype)

    _attend((tq // rows) * n_chunks, task, emit, want_lse=True)


def _attn_a(qkv, sink_lanes, batch, seq):
    blk = A_HALF_WINDOW
    tq = A_BLOCKS_PER_STEP * blk
    per = tq // blk
    nblk = seq // blk
    kcol, vcol = A_Q // A_KV, A_Q // A_KV + 1

    def main(col):
        return pl.BlockSpec((None, tq, A_KV), lambda b, i: (b, i, col))

    def left(col):
        return pl.BlockSpec((None, blk, A_KV), lambda b, i: (b, jnp.maximum(i * per - 1, 0), col))

    def right(col):
        return pl.BlockSpec((None, blk, A_KV),
                            lambda b, i: (b, jnp.minimum((i + 1) * per, nblk - 1), col))

    x = qkv.reshape(batch, seq, A_W)
    mask = _band_mask_table(blk, 3 * blk, blk)
    out = pl.pallas_call(
        functools.partial(_attn_a_kernel, tq=tq, seq=seq),
        grid=(batch, seq // tq),
        in_specs=[pl.BlockSpec((None, tq, A_Q), lambda b, i: (b, i, 0)),
                  left(kcol), main(kcol), right(kcol),
                  left(vcol), main(vcol), right(vcol),
                  _const_spec((1, A_Q)), _const_spec(mask.shape)],
        out_specs=pl.BlockSpec((None, tq, A_Q), lambda b, i: (b, i, 0)),
        out_shape=jax.ShapeDtypeStruct((batch, seq, A_Q), BF16),
        compiler_params=_params(2),
        name="attn_window",
    )(x, x, x, x, x, x, x, sink_lanes, mask)
    return out.reshape(batch * seq, A_Q)


def _attn_b_kernel(q_ref, kl_ref, km_ref, kr_ref, vl_ref, vm_ref, vr_ref, mask_ref, o_ref, lse_ref,
                   *, tq, sub, halo, dil):
    i = pl.program_id(1)
    r = pl.program_id(2)
    blk = 2 * halo
    n_chunks = B_G // LANES
    kcat = jnp.concatenate([kl_ref[...], km_ref[...], kr_ref[...]], axis=0)
    vcat = jnp.concatenate([vl_ref[...], vm_ref[...], vr_ref[...]], axis=0)
    variants = [_band_variant(i * tq + j * blk, blk, sub) for j in range(tq // blk)]

    def task(t):
        j, p = divmod(t, n_chunks)
        win = slice(j * blk, (j + 2) * blk)
        cols = slice(p * LANES, (p + 1) * LANES)
        return _Task(q=lambda: q_ref[j * blk:(j + 1) * blk, cols],
                     keys=lambda: kcat[win, cols], vals=lambda: vcat[win, cols],
                     mask=lambda c0, c1: mask_ref[variants[j], :, c0:c1],
                     mask_cols=((0, 2 * blk),), bias=None)

    def emit(t, o, lse):
        j, p = divmod(t, n_chunks)
        if dil == 1:
            rows = slice(j * blk, (j + 1) * blk)
        else:
            rows = pl.ds(j * blk * dil + r, blk, stride=dil)
        o_ref[p, rows, :] = o
        lse_ref[p, rows, :] = lse

    _attend((tq // blk) * n_chunks, task, emit, want_lse=True)


def _attn_b_group(qkv, group, batch, seq):
    window, dil = B_GROUPS[group]
    halo = window // (2 * dil)
    sub = seq // dil
    tq = min(4 * halo if dil > 4 else 8 * halo, sub)
    per = tq // halo
    nhalo = sub // halo

    def main(col):
        return pl.BlockSpec((None, None, tq, B_G), lambda b, i, r: (b, r, i, col))

    def left(col):
        return pl.BlockSpec((None, None, halo, B_G),
                            lambda b, i, r: (b, r, jnp.maximum(i * per - 1, 0), col))

    def right(col):
        return pl.BlockSpec((None, None, halo, B_G),
                            lambda b, i, r: (b, r, jnp.minimum((i + 1) * per, nhalo - 1), col))

    x = qkv.reshape(batch, dil, sub, B_GW)
    n_chunks = B_G // LANES
    out_spec = pl.BlockSpec((None, n_chunks, tq * dil, LANES), lambda b, i, r: (b, 0, i, 0))
    out_shape = jax.ShapeDtypeStruct((batch, n_chunks, seq, LANES), F32)
    mask = _band_mask_table(2 * halo, 4 * halo, halo)
    o, lse = pl.pallas_call(
        functools.partial(_attn_b_kernel, tq=tq, sub=sub, halo=halo, dil=dil),
        grid=(batch, sub // tq, dil),
        in_specs=[main(0), left(1), main(1), right(1), left(2), main(2), right(2),
                  _const_spec(mask.shape)],
        out_specs=[out_spec, out_spec],
        out_shape=[out_shape, out_shape],
        compiler_params=_params(3),
        name=f"attn_dilated_{dil}",
    )(x, x, x, x, x, x, x, mask)
    return o, lse


def _attn_c_kernel(q_ref, kp_ref, kc_ref, kn_ref, vp_ref, vc_ref, vn_ref, bias_ref, mask_ref, o_ref,
                   kcat, vcat, *, rows):
    i = pl.program_id(1)
    halo_tokens = C_HALO_ROWS * GRID_W
    step_tokens = C_ROWS_PER_STEP * GRID_W
    win = NA_ROWS * GRID_W
    n_chunks = C_Q // LANES
    for cat, (before, here, after) in ((kcat, (kp_ref, kc_ref, kn_ref)), (vcat, (vp_ref, vc_ref, vn_ref))):
        cat[0:halo_tokens] = before[...]
        cat[halo_tokens:halo_tokens + step_tokens] = here[...]
        cat[halo_tokens + step_tokens:] = after[...]
    starts, offs = [], []
    for rr in range(C_ROWS_PER_STEP):
        r = i * C_ROWS_PER_STEP + rr
        rstart = jnp.clip(r - NA_ROWS // 2, 0, rows - NA_ROWS)
        starts.append(rstart - r + (NA_ROWS - 1))
        offs.append(pl.multiple_of((rstart - i * C_ROWS_PER_STEP + C_HALO_ROWS) * GRID_W, GRID_W))

    def task(t):
        rr, p = divmod(t, n_chunks)
        cols = slice(p * LANES, (p + 1) * LANES)
        return _Task(q=lambda: q_ref[rr * GRID_W:(rr + 1) * GRID_W, cols],
                     keys=lambda: kcat[pl.ds(offs[rr], win), cols],
                     vals=lambda: vcat[pl.ds(offs[rr], win), cols],
                     mask=lambda c0, c1: mask_ref[:, c0:c1], mask_cols=((0, win),),
                     bias=lambda: bias_ref[starts[rr], p])

    def emit(t, o, lse):
        rr, p = divmod(t, n_chunks)
        o_ref[rr * GRID_W:(rr + 1) * GRID_W, p * LANES:(p + 1) * LANES] = o.astype(o_ref.dtype)

    _attend(C_ROWS_PER_STEP * n_chunks, task, emit, want_lse=False)


def _attn_c(qkv, bias, batch, seq):
    rows = seq // GRID_W
    assert rows >= NA_ROWS and rows % C_ROWS_PER_STEP == 0
    step_tokens = C_ROWS_PER_STEP * GRID_W
    halo_tokens = C_HALO_ROWS * GRID_W
    per = step_tokens // halo_tokens
    nsteps = rows // C_ROWS_PER_STEP
    nhalo = seq // halo_tokens

    def prev(col):
        return pl.BlockSpec((None, halo_tokens, C_Q),
                            lambda b, i: (b, jnp.maximum(i * per - 1, 0), col))

    def cur(col):
        return pl.BlockSpec((None, step_tokens, C_Q), lambda b, i: (b, i, col))

    def nxt(col):
        return pl.BlockSpec((None, halo_tokens, C_Q),
                            lambda b, i: (b, jnp.minimum((i + 1) * per, nhalo - 1), col))

    x = qkv.reshape(batch, seq, C_W)
    cat_tokens = step_tokens + 2 * halo_tokens
    c = jnp.arange(GRID_W)
    cstart = jnp.clip(c - NA_COLS // 2, 0, GRID_W - NA_COLS)[:, None]
    colmask = ((c[None, :] >= cstart) & (c[None, :] < cstart + NA_COLS)).astype(F32)
    mask = jnp.tile(colmask, (1, NA_ROWS))
    out = pl.pallas_call(
        functools.partial(_attn_c_kernel, rows=rows),
        grid=(batch, nsteps),
        in_specs=[cur(0), prev(1), cur(1), nxt(1), prev(2), cur(2), nxt(2),
                  _const_spec(bias.shape), _const_spec(mask.shape)],
        out_specs=cur(0),
        out_shape=jax.ShapeDtypeStruct((batch, seq, C_Q), BF16),
        scratch_shapes=[pltpu.VMEM((cat_tokens, C_Q), BF16), pltpu.VMEM((cat_tokens, C_Q), BF16)],
        compiler_params=_params(2),
        name="attn_neighbourhood",
    )(x, x, x, x, x, x, x, bias, mask)
    return out.reshape(batch * seq, C_Q)


def _c_bias_table(rpb):
    c = jnp.arange(GRID_W)
    coff = jnp.clip(c[None, :] - c[:, None] + (NA_COLS - 1), 0, 2 * NA_COLS - 2)
    t = rpb[:, :, coff]
    tiles = [t[:, st:st + NA_ROWS].transpose(0, 2, 1, 3).reshape(C_HEADS // 2, 2 * GRID_W, NA_ROWS * GRID_W)
             for st in range(NA_ROWS)]
    return jnp.stack(tiles).astype(F32)


def _merge_kernel(x_ref, oa_ref, o0_ref, l0_ref, o1_ref, l1_ref, o2_ref, l2_ref, oc_ref,
                  g_ref, wg_ref, wpa_ref, wpb_ref, wpc_ref, wo_ref, out_ref):
    x = x_ref[...]
    hn = _rms(x, g_ref[...]).astype(BF16)
    ob = []
    for p in range(B_G // LANES):
        l0, l1, l2 = l0_ref[p], l1_ref[p], l2_ref[p]
        m = jnp.maximum(jnp.maximum(l0, l1), l2)
        e0, e1, e2 = jnp.exp(l0 - m), jnp.exp(l1 - m), jnp.exp(l2 - m)
        inv = 1.0 / (e0 + e1 + e2)
        ob.append(((e0 * inv) * o0_ref[p] + (e1 * inv) * o1_ref[p] + (e2 * inv) * o2_ref[p]).astype(BF16))
    ob = jnp.concatenate(ob, axis=1)
    oa, oc = oa_ref[...], oc_ref[...]
    chunk = MERGE_CHUNK
    branches = ((oa, wpa_ref, 0), (oc, wpc_ref, 2), (ob, wpb_ref, 1))

    def gated_mix(n):
        cols = slice(n * chunk, (n + 1) * chunk)
        mix = None
        for o, w_ref, k in branches:
            y = jnp.dot(o, w_ref[:, cols], preferred_element_type=F32)
            gcols = slice(k * D_MODEL + n * chunk, k * D_MODEL + (n + 1) * chunk)
            gate = _sigmoid(jnp.dot(hn, wg_ref[:, gcols], preferred_element_type=F32))
            mix = gate * y if mix is None else mix + gate * y
        return mix.astype(BF16)

    n_chunks = D_MODEL // chunk
    acc = x
    ahead = gated_mix(0)
    for n in range(n_chunks):
        mix = ahead
        if n + 1 < n_chunks:
            ahead = gated_mix(n + 1)
        acc = acc + jnp.dot(mix, wo_ref[n * chunk:(n + 1) * chunk, :], preferred_element_type=F32)
    out_ref[...] = acc


def _merge(x, oa, ob_parts, oc, g, wg, wpa, wpb, wpc, wo, seq):
    tokens = x.shape[0]
    tm = TOKEN_TILE
    pos_blocks = seq // tm

    def tok(width):
        return pl.BlockSpec((tm, width), lambda i: (i, 0))

    b_spec = pl.BlockSpec((None, B_G // LANES, tm, LANES),
                          lambda i: (i // pos_blocks, 0, i % pos_blocks, 0))
    return pl.pallas_call(
        _merge_kernel,
        grid=(tokens // tm,),
        in_specs=[tok(D_MODEL), tok(A_Q)] + [b_spec] * 6 + [tok(C_Q),
                  _const_spec((1, D_MODEL)), _const_spec((D_MODEL, GATE_W)),
                  _const_spec((A_Q, D_MODEL)), _const_spec((B_G, D_MODEL)),
                  _const_spec((C_Q, D_MODEL)), _const_spec((D_MODEL, D_MODEL))],
        out_specs=tok(D_MODEL),
        out_shape=jax.ShapeDtypeStruct((tokens, D_MODEL), F32),
        compiler_params=_params(1),
        name="merge",
    )(x, oa, *ob_parts, oc, g, wg, wpa, wpb, wpc, wo)


def _mlp_kernel(x_ref, g_ref, wup_ref, wdown_ref, gf_ref, out_ref, *, final):
    x = x_ref[...]
    hn = _rms(x, g_ref[...]).astype(BF16)
    chunk = D_MODEL
    acc = x
    for n in range(D_FF // chunk):
        h = jnp.dot(hn, wup_ref[:, n * chunk:(n + 1) * chunk], preferred_element_type=F32)
        h = jnp.square(jnp.maximum(h, 0.0)).astype(BF16)
        acc = acc + jnp.dot(h, wdown_ref[n * chunk:(n + 1) * chunk, :], preferred_element_type=F32)
    out_ref[...] = _rms(acc, gf_ref[...]) if final else acc


def _mlp(x, g, wup, wdown, gf, final):
    tokens = x.shape[0]
    tm = TOKEN_TILE
    return pl.pallas_call(
        functools.partial(_mlp_kernel, final=final),
        grid=(tokens // tm,),
        in_specs=[pl.BlockSpec((tm, D_MODEL), lambda i: (i, 0)),
                  _const_spec((1, D_MODEL)), _const_spec((D_MODEL, D_FF)),
                  _const_spec((D_FF, D_MODEL)), _const_spec((1, D_MODEL))],
        out_specs=pl.BlockSpec((tm, D_MODEL), lambda i: (i, 0)),
        out_shape=jax.ShapeDtypeStruct((tokens, D_MODEL), F32),
        compiler_params=_params(1),
        name="mlp_final" if final else "mlp",
    )(x, g, wup, wdown, gf)


def _rope_tables(seq):
    half = ROPE_DIM // 2
    inv = ROPE_THETA ** (-jnp.arange(0, ROPE_DIM, 2, dtype=F32) / ROPE_DIM)
    ang = jnp.arange(seq, dtype=F32)[:, None] * inv[None, :]
    cos, sin = jnp.cos(ang), jnp.sin(ang)
    pad = HEAD_DIM - ROPE_DIM
    ones = jnp.ones((seq, pad), F32)
    zeros = jnp.zeros((seq, pad), F32)
    zh = jnp.zeros((seq, half), F32)
    rc = jnp.concatenate([cos, cos, ones], axis=1)
    rsa = jnp.concatenate([-sin, zh, zeros], axis=1)
    rsb = jnp.concatenate([zh, sin, zeros], axis=1)
    return tuple(jnp.tile(t, (1, LANES // HEAD_DIM)) for t in (rc, rsa, rsb))


def _qkv_column_ranges():
    b_q = A_W
    b_span = len(B_GROUPS) * B_G
    ranges = [(0, A_W)]
    for g in range(len(B_GROUPS)):
        for part in range(3):
            start = b_q + part * b_span + g * B_G
            ranges.append((start, start + B_G))
    ranges.append((b_q + 3 * b_span, QKV_W))
    return ranges


def _trunk(x, weights, batch, seq):
    (g1, w_qkv, w_gate, sink_lanes, bias_c, w_pa, w_pb, w_pc, w_o, g2, w_up, w_down, gf) = weights
    rope = _rope_tables(seq)
    h = x.reshape(batch * seq, D_MODEL)
    for layer in range(DEPTH):
        qkv = _qkv_proj(h, g1[layer], w_qkv[layer], rope, batch, seq)
        qkv_a, qkv_b, qkv_c = qkv[0], qkv[1:-1], qkv[-1]
        oa = _attn_a(qkv_a, sink_lanes[layer], batch, seq)
        ob_parts = []
        for group in range(len(B_GROUPS)):
            ob_parts.extend(_attn_b_group(qkv_b[group], group, batch, seq))
        oc = _attn_c(qkv_c, bias_c[layer], batch, seq)
        h = _merge(h, oa, ob_parts, oc, g1[layer], w_gate[layer], w_pa[layer], w_pb[layer],
                   w_pc[layer], w_o[layer], seq)
        h = _mlp(h, g2[layer], w_up[layer], w_down[layer], gf, layer == DEPTH - 1)
    return h.reshape(batch, seq, D_MODEL)


def _prepare_weights(norm1_g, w_in, sink_a, rpb_c, w_pa, w_pb, w_pc, w_o, norm2_g, w_up, w_down, final_g):
    w_qkv = jnp.concatenate([w_in[:, :, a:b].astype(BF16) for a, b in _qkv_column_ranges()], axis=2)
    w_gate = w_in[:, :, QKV_W:].astype(BF16)
    sink_lanes = jnp.repeat(sink_a.astype(F32), HEAD_DIM, axis=1)[:, None, :]
    bias_c = jax.vmap(_c_bias_table)(rpb_c)
    return (norm1_g.astype(F32)[:, None, :], w_qkv, w_gate, sink_lanes, bias_c,
            w_pa.astype(BF16), w_pb.astype(BF16), w_pc.astype(BF16), w_o.astype(BF16),
            norm2_g.astype(F32)[:, None, :], w_up.astype(BF16), w_down.astype(BF16),
            final_g.astype(F32)[None, :])


def kernel(x_prompt, x_sample, norm1_g, w_in, sink_a, rpb_c, w_pa, w_pb, w_pc, w_o, norm2_g, w_up, w_down,
           final_g):
    weights = _prepare_weights(norm1_g, w_in, sink_a, rpb_c, w_pa, w_pb, w_pc, w_o, norm2_g, w_up,
                               w_down, final_g)
    y_prompt = _trunk(x_prompt, weights, *x_prompt.shape[:2])
    y_sample = _trunk(x_sample, weights, *x_sample.shape[:2])
    return (y_prompt, y_sample)
```
